```python
import jax, jax.numpy as jnp
from jax import lax
import numpy as np

D_MODEL = 2048
BATCH = 2
SEQ = 16384
DEPTH = 4

CHUNK = 64
Q_BLOCK = 128
N_MIXERS = 3
N_HEADS = 4
HEAD_DIM = 256
ATTN_WIDTH = N_HEADS * HEAD_DIM
DIFF_QK_DIM = HEAD_DIM // 2
D_FF = -(-8 * D_MODEL // (3 * 256)) * 256
RMS_EPS = 1e-6
DIFF_EPS = 1e-5
ALIBI_MAX_EXP = 8.0
HI = lax.Precision.HIGHEST

kernel_name = "chunk_causal_hybrid_fox_stickbreak_diffattn"


def _rmsnorm(x, g, eps=RMS_EPS):
    xf = x.astype(jnp.float32)
    y = xf * lax.rsqrt(jnp.mean(xf * xf, axis=-1, keepdims=True) + eps)
    return (y * g.astype(jnp.float32)).astype(x.dtype)


def _heads(a):
    b, s, _ = a.shape
    return a.reshape(b, s, N_HEADS, -1).transpose(0, 2, 1, 3)


def _merge_heads(o):
    b, h, s, dv = o.shape
    return o.transpose(0, 2, 1, 3).reshape(b, s, h * dv)


def _positions(t0, L):
    t = t0 + jnp.arange(Q_BLOCK, dtype=jnp.int32)
    kpos = jnp.arange(L, dtype=jnp.int32)
    return t, kpos


def _forgetting_attention(h, w_in, b_f, w_out):
    b, s, _ = h.shape
    A = ATTN_WIDTH
    proj = h @ w_in
    q = _heads(proj[..., :A]) * (HEAD_DIM ** -0.5)
    k = _heads(proj[..., A:2 * A])
    v = _heads(proj[..., 2 * A:3 * A])
    f_logit = proj[..., 3 * A:].astype(jnp.float32) + b_f.astype(jnp.float32)
    cum = jnp.cumsum(jax.nn.log_sigmoid(f_logit), axis=1).transpose(0, 2, 1)
    outs = []
    for t0 in range(0, s, Q_BLOCK):
        L = t0 + Q_BLOCK
        t, kpos = _positions(t0, L)
        logits = (jnp.einsum('bhqd,bhkd->bhqk', q[:, :, t0:L], k[:, :, :L]).astype(jnp.float32)
                  + cum[:, :, t0:L, None] - cum[:, :, None, :L])
        mask = kpos[None, :] <= t[:, None]
        p = jax.nn.softmax(jnp.where(mask, logits, -jnp.inf), axis=-1)
        outs.append(jnp.einsum('bhqk,bhkd->bhqd', p.astype(v.dtype), v[:, :, :L]))
    return _merge_heads(jnp.concatenate(outs, axis=2)) @ w_out


def _stick_breaking_attention(h, w_in, w_out):
    b, s, _ = h.shape
    A = ATTN_WIDTH
    proj = h @ w_in
    q = _heads(proj[..., :A]) * (HEAD_DIM ** -0.5)
    k = _heads(proj[..., A:2 * A])
    v = _heads(proj[..., 2 * A:])
    upper = jnp.tril(jnp.ones((Q_BLOCK, Q_BLOCK), jnp.float32), -1)
    outs = []
    for t0 in range(0, s, Q_BLOCK):
        L = t0 + Q_BLOCK
        n = L // Q_BLOCK
        t, kpos = _positions(t0, L)
        z = jnp.einsum('bhqd,bhkd->bhqk', q[:, :, t0:L], k[:, :, :L]).astype(jnp.float32)
        mask = kpos[None, :] < t[:, None]
        lom = jnp.where(mask, jax.nn.log_sigmoid(-z), 0.0)
        lom5 = lom.reshape(b, N_HEADS, Q_BLOCK, n, Q_BLOCK)
        inner = jnp.einsum('bhqnk,kj->bhqnj', lom5, upper, precision=HI)
        later = jnp.tril(jnp.ones((n, n), jnp.float32), -1)
        cross = jnp.einsum('bhqm,mn->bhqn', lom5.sum(-1), later, precision=HI)
        tail = (inner + cross[..., None]).reshape(b, N_HEADS, Q_BLOCK, L)
        a = jnp.where(mask, jnp.exp(jax.nn.log_sigmoid(z) + tail), 0.0)
        outs.append(jnp.einsum('bhqk,bhkd->bhqd', a.astype(v.dtype), v[:, :, :L]))
    return _merge_heads(jnp.concatenate(outs, axis=2)) @ w_out


def _differential_attention(h, w_in, lq1, lk1, lq2, lk2, subln, w_out, lambda_init):
    b, s, _ = h.shape
    A = ATTN_WIDTH
    f32 = jnp.float32
    proj = h @ w_in
    q = proj[..., :A].reshape(b, s, N_HEADS, 2, DIFF_QK_DIM).transpose(3, 0, 2, 1, 4) * (DIFF_QK_DIM ** -0.5)
    k = proj[..., A:2 * A].reshape(b, s, N_HEADS, 2, DIFF_QK_DIM).transpose(3, 0, 2, 1, 4)
    v = _heads(proj[..., 2 * A:])
    lam = (jnp.exp(jnp.sum(lq1.astype(f32) * lk1.astype(f32)))
           - jnp.exp(jnp.sum(lq2.astype(f32) * lk2.astype(f32))) + lambda_init)
    slopes = jnp.exp2(-ALIBI_MAX_EXP * jnp.arange(1, N_HEADS + 1, dtype=f32) / N_HEADS)
    outs = []
    for t0 in range(0, s, Q_BLOCK):
        L = t0 + Q_BLOCK
        t, kpos = _positions(t0, L)
        dist = jnp.abs(t[:, None] - kpos[None, :]).astype(f32)
        bias = -slopes[:, None, None] * dist[None]
        mask = (kpos // CHUNK)[None, :] <= (t // CHUNK)[:, None]

        def probs(qq, kk):
            logits = jnp.einsum('bhqd,bhkd->bhqk', qq[:, :, t0:L], kk[:, :, :L]).astype(f32) + bias
            return jax.nn.softmax(jnp.where(mask, logits, -jnp.inf), axis=-1)

        a = probs(q[0], k[0]) - lam * probs(q[1], k[1])
        outs.append(jnp.einsum('bhqk,bhkd->bhqd', a.astype(v.dtype), v[:, :, :L]))
    o = _rmsnorm(jnp.concatenate(outs, axis=2), subln, DIFF_EPS) * (1.0 - lambda_init)
    return _merge_heads(o) @ w_out


def _swiglu(h, w_gate_up, w_down):
    gu = h @ w_gate_up
    g, u = gu[..., :D_FF], gu[..., D_FF:]
    return (jax.nn.silu(g) * u) @ w_down


def setup_inputs(seed: int = 0) -> dict:
    key = jax.random.key(seed)
    keys = jax.random.split(key, 20)
    D, H, A = D_MODEL, N_HEADS, ATTN_WIDTH
    n_fox = len(range(0, DEPTH, N_MIXERS))
    n_sb = len(range(1, DEPTH, N_MIXERS))
    n_diff = len(range(2, DEPTH, N_MIXERS))

    def nrm(i, shape, scale):
        return jax.random.normal(keys[i], shape, jnp.float32) * scale

    return {
        "x": nrm(0, (BATCH, SEQ, D), 1.0),
        "norm_mix": 1.0 + nrm(1, (DEPTH, D), 0.02),
        "fox_w_in": nrm(2, (n_fox, D, 3 * A + H), D ** -0.5),
        "fox_b_f": 2.0 + nrm(3, (n_fox, H), 0.5),
        "fox_w_out": nrm(4, (n_fox, A, D), A ** -0.5),
        "sb_w_in": nrm(5, (n_sb, D, 3 * A), D ** -0.5),
        "sb_w_out": nrm(6, (n_sb, A, D), A ** -0.5),
        "diff_w_in": nrm(7, (n_diff, D, 3 * A), D ** -0.5),
        "diff_lambda_q1": nrm(8, (n_diff, DIFF_QK_DIM), 0.1),
        "diff_lambda_k1": nrm(9, (n_diff, DIFF_QK_DIM), 0.1),
        "diff_lambda_q2": nrm(10, (n_diff, DIFF_QK_DIM), 0.1),
        "diff_lambda_k2": nrm(11, (n_diff, DIFF_QK_DIM), 0.1),
        "diff_subln": 1.0 + nrm(12, (n_diff, HEAD_DIM), 0.02),
        "diff_w_out": nrm(13, (n_diff, A, D), A ** -0.5),
        "norm_ffn": 1.0 + nrm(14, (DEPTH, D), 0.02),
        "ffn_w_gate_up": nrm(15, (DEPTH, D, 2 * D_FF), D ** -0.5),
        "ffn_w_down": nrm(16, (DEPTH, D_FF, D), D_FF ** -0.5),
        "final_norm": 1.0 + nrm(17, (D,), 0.02),
    }


def reference(x, norm_mix, fox_w_in, fox_b_f, fox_w_out, sb_w_in, sb_w_out,
              diff_w_in, diff_lambda_q1, diff_lambda_k1, diff_lambda_q2, diff_lambda_k2,
              diff_subln, diff_w_out, norm_ffn, ffn_w_gate_up, ffn_w_down, final_norm):
    for i in range(DEPTH):
        kind, j = i % N_MIXERS, i // N_MIXERS
        h = _rmsnorm(x, norm_mix[i])
        if kind == 0:
            y = _forgetting_attention(h, fox_w_in[j], fox_b_f[j], fox_w_out[j])
        elif kind == 1:
            y = _stick_breaking_attention(h, sb_w_in[j], sb_w_out[j])
        else:
            lambda_init = 0.8 - 0.6 * float(np.exp(-0.3 * i))
            y = _differential_attention(h, diff_w_in[j], diff_lambda_q1[j], diff_lambda_k1[j],
                                        diff_lambda_q2[j], diff_lambda_k2[j], diff_subln[j],
                                        diff_w_out[j], lambda_init)
        x = x + y
        h = _rmsnorm(x, norm_ffn[i])
        x = x + _swiglu(h, ffn_w_gate_up[i], ffn_w_down[i])
    return _rmsnorm(x, final_norm)
```

```python
import functools

import numpy as np
import jax
import jax.numpy as jnp
from jax import lax
from jax.experimental import pallas as pl
from jax.experimental.pallas import tpu as pltpu

N_MIXERS = 3
N_HEADS = 4
HEAD_DIM = 256
ATTN_WIDTH = N_HEADS * HEAD_DIM
DIFF_QK_DIM = HEAD_DIM // 2
CHUNK = 64
RMS_EPS = 1e-6
DIFF_EPS = 1e-5
ALIBI_MAX_EXP = 8.0

LANES = 128
VMEM_LIMIT_BYTES = 56 * 1024 * 1024

BF16 = jnp.bfloat16
F32 = jnp.float32
NEG_INF = float("-inf")


def _params(semantics):
    return pltpu.CompilerParams(dimension_semantics=semantics, vmem_limit_bytes=VMEM_LIMIT_BYTES)


def _dot(a, b):
    return jnp.dot(a, b, preferred_element_type=F32)


def _dot_nt(a, b):
    return lax.dot_general(a, b, (((1,), (1,)), ((), ())), preferred_element_type=F32)


def _rms_rows(x, g, eps):
    return x * lax.rsqrt(jnp.mean(x * x, axis=-1, keepdims=True) + eps) * g


def _norm_proj_kernel(x_ref, g_ref, w_ref, cs_ref, *rest, has_gate):
    if has_gate:
        wg_ref, o_ref, og_ref, h_ref = rest
    else:
        o_ref, h_ref = rest

    @pl.when(pl.program_id(1) == 0)
    def _():
        h = _rms_rows(x_ref[...], g_ref[...], RMS_EPS).astype(BF16)
        h_ref[...] = h
        if has_gate:
            og_ref[...] = _dot(h, wg_ref[...])

    o_ref[...] = (_dot(h_ref[...], w_ref[...]) * cs_ref[...]).astype(o_ref.dtype)


def _norm_proj(x2, g, w, colscale, w_gate=None, *, tm=512, tn=1024):
    m, d = x2.shape
    n = w.shape[1]
    has_gate = w_gate is not None
    in_specs = [
        pl.BlockSpec((tm, d), lambda i, j: (i, 0)),
        pl.BlockSpec((1, d), lambda i, j: (0, 0)),
        pl.BlockSpec((d, tn), lambda i, j: (0, j)),
        pl.BlockSpec((1, tn), lambda i, j: (0, j)),
    ]
    args = [x2, g.reshape(1, d), w, colscale.reshape(1, n)]
    out_shape = [jax.ShapeDtypeStruct((m, n), BF16)]
    out_specs = [pl.BlockSpec((tm, tn), lambda i, j: (i, j))]
    if has_gate:
        in_specs.append(pl.BlockSpec((d, LANES), lambda i, j: (0, 0)))
        args.append(w_gate)
        out_shape.append(jax.ShapeDtypeStruct((m, LANES), F32))
        out_specs.append(pl.BlockSpec((tm, LANES), lambda i, j: (i, 0)))
    return pl.pallas_call(
        functools.partial(_norm_proj_kernel, has_gate=has_gate),
        grid=(m // tm, n // tn),
        in_specs=in_specs,
        out_specs=out_specs,
        out_shape=out_shape,
        scratch_shapes=[pltpu.VMEM((tm, d), BF16)],
        compiler_params=_params(("arbitrary", "arbitrary")),
        name="norm_proj_gate" if has_gate else "norm_proj",
    )(*args)


def _split3(x):
    hi = x.astype(BF16)
    r = x - hi.astype(F32)
    mid = r.astype(BF16)
    lo = (r - mid.astype(F32)).astype(BF16)
    return hi, mid, lo


def _log_sigmoid(x):
    return -(jnp.maximum(-x, 0.0) + jnp.log1p(jnp.exp(-jnp.abs(x))))


def _gate_cumsum_kernel(gl_ref, bf_ref, o_ref, carry_ref, *, tiles_per_batch):
    i = pl.program_id(0)

    @pl.when(i % tiles_per_batch == 0)
    def _():
        carry_ref[...] = jnp.zeros_like(carry_ref)

    ls = _log_sigmoid(gl_ref[...] + bf_ref[...])
    tm = ls.shape[0]
    row = lax.broadcasted_iota(jnp.int32, (tm, tm), 0)
    col = lax.broadcasted_iota(jnp.int32, (tm, tm), 1)
    lower = jnp.where(col <= row, 1.0, 0.0).astype(BF16)
    hi, mid, lo = _split3(ls)
    cum = _dot(lower, hi) + _dot(lower, mid) + _dot(lower, lo) + carry_ref[...]
    o_ref[...] = cum
    carry_ref[...] = cum[tm - 1:tm, :]


def _gate_cumsum(gate_logits, b_f_row, seq, *, tm=512):
    m = gate_logits.shape[0]
    return pl.pallas_call(
        functools.partial(_gate_cumsum_kernel, tiles_per_batch=seq // tm),
        grid=(m // tm,),
        in_specs=[pl.BlockSpec((tm, LANES), lambda i: (i, 0)),
                  pl.BlockSpec((1, LANES), lambda i: (0, 0))],
        out_specs=pl.BlockSpec((tm, LANES), lambda i: (i, 0)),
        out_shape=jax.ShapeDtypeStruct((m, LANES), F32),
        scratch_shapes=[pltpu.VMEM((1, LANES), F32)],
        compiler_params=_params(("arbitrary",)),
        name="gate_cumsum",
    )(gate_logits, b_f_row)


def _qkv_specs(seq, tq):
    return [
        pl.BlockSpec((None, tq, HEAD_DIM), lambda b, h, i: (b, i, h)),
        pl.BlockSpec((None, seq, HEAD_DIM), lambda b, h, i: (b, 0, N_HEADS + h)),
        pl.BlockSpec((None, seq, HEAD_DIM), lambda b, h, i: (b, 0, 2 * N_HEADS + h)),
    ]


def _out_spec(tq):
    return pl.BlockSpec((None, tq, HEAD_DIM), lambda b, h, i: (b, i, h))


def _softmax_step(s, v, m, l, acc):
    m_new = jnp.maximum(m, jnp.max(s, axis=-1, keepdims=True))
    alpha = jnp.exp(m - m_new)
    p = jnp.exp(s - m_new)
    l = alpha * l + jnp.sum(p, axis=-1, keepdims=True)
    acc = alpha * acc + _dot(p.astype(BF16), v)
    return m_new, l, acc


def _fox_kernel(q_ref, k_ref, v_ref, ck_ref, cq_ref, o_ref, *, tq):
    i = pl.program_id(2)
    q = q_ref[...]
    c0 = cq_ref[:, 0:1]

    def step(start, m, l, acc, masked):
        k = k_ref[pl.ds(start, tq), :]
        v = v_ref[pl.ds(start, tq), :]
        s = _dot_nt(q, k) + (c0 - ck_ref[:, pl.ds(start, tq)])
        if masked:
            row = lax.broadcasted_iota(jnp.int32, (tq, tq), 0)
            col = lax.broadcasted_iota(jnp.int32, (tq, tq), 1)
            s = jnp.where(col <= row, s, NEG_INF)
        return _softmax_step(s, v, m, l, acc)

    init = (jnp.full((tq, 1), NEG_INF, F32), jnp.zeros((tq, 1), F32), jnp.zeros((tq, HEAD_DIM), F32))
    carry = step(pl.multiple_of(i * tq, tq), *init, masked=True)

    def body(j, carry):
        return step(pl.multiple_of(j * tq, tq), *carry, masked=False)

    m, l, acc = lax.fori_loop(0, i, body, carry)
    o_ref[...] = (acc * (1.0 / l)).astype(o_ref.dtype)


def _fox_attention(proj, cum, *, tq=512):
    b, seq, _ = proj.shape
    cum4 = cum.reshape(b, N_HEADS, 1, seq)
    return pl.pallas_call(
        functools.partial(_fox_kernel, tq=tq),
        grid=(b, N_HEADS, seq // tq),
        in_specs=_qkv_specs(seq, tq) + [
            pl.BlockSpec((None, None, 1, seq), lambda b, h, i: (b, h, 0, 0)),
            pl.BlockSpec((None, None, 1, tq), lambda b, h, i: (b, h, 0, i)),
        ],
        out_specs=_out_spec(tq),
        out_shape=jax.ShapeDtypeStruct((b, seq, ATTN_WIDTH), BF16),
        compiler_params=_params(("arbitrary", "arbitrary", "arbitrary")),
        name="fox_attention",
    )(proj, proj, proj, cum4, cum4)


def _split2(x):
    hi = x.astype(BF16)
    lo = (x - hi.astype(F32)).astype(BF16)
    return hi, lo


def _sb_kernel(q_ref, k_ref, v_ref, o_ref, *, tq, tk):
    i = pl.program_id(2)
    q = q_ref[...]
    row_k = lax.broadcasted_iota(jnp.int32, (tk, tk), 0)
    col_k = lax.broadcasted_iota(jnp.int32, (tk, tk), 1)
    suffix = jnp.where(row_k > col_k, 1.0, 0.0).astype(BF16)

    def step(start, tail, acc, diag_offset):
        k = k_ref[pl.ds(start, tk), :]
        v = v_ref[pl.ds(start, tk), :]
        z = _dot_nt(q, k)
        lom = -(jnp.maximum(z, 0.0) + jnp.log1p(jnp.exp(-jnp.abs(z))))
        if diag_offset is not None:
            row = lax.broadcasted_iota(jnp.int32, (tq, tk), 0)
            col = lax.broadcasted_iota(jnp.int32, (tq, tk), 1) + diag_offset
            mask = col < row
            lom = jnp.where(mask, lom, 0.0)
        hi, lo = _split2(lom)
        inner = _dot(hi, suffix) + _dot(lo, suffix)
        a = jnp.exp(lom + z + inner + tail)
        if diag_offset is not None:
            a = jnp.where(mask, a, 0.0)
        acc = acc + _dot(a.astype(BF16), v)
        tail = tail + jnp.sum(lom, axis=-1, keepdims=True)
        return tail, acc

    carry = (jnp.zeros((tq, 1), F32), jnp.zeros((tq, HEAD_DIM), F32))
    for d in reversed(range(tq // tk)):
        carry = step(pl.multiple_of(i * tq + d * tk, tk), *carry, diag_offset=d * tk)

    n_full = i * (tq // tk)

    def body(j, carry):
        return step(pl.multiple_of((n_full - 1 - j) * tk, tk), *carry, diag_offset=None)

    _, acc = lax.fori_loop(0, n_full, body, carry)
    o_ref[...] = acc.astype(o_ref.dtype)


def _sb_attention(proj, *, tq=512, tk=256):
    b, seq, _ = proj.shape
    return pl.pallas_call(
        functools.partial(_sb_kernel, tq=tq, tk=tk),
        grid=(b, N_HEADS, seq // tq),
        in_specs=_qkv_specs(seq, tq),
        out_specs=_out_spec(tq),
        out_shape=jax.ShapeDtypeStruct((b, seq, ATTN_WIDTH), BF16),
        compiler_params=_params(("arbitrary", "arbitrary", "arbitrary")),
        name="sb_attention",
    )(proj, proj, proj)


def _diff_kernel(slopes_ref, q_ref, k_ref, v_ref, lq1_ref, lk1_ref, lq2_ref, lk2_ref, g_ref, o_ref,
                 *, tq, lambda_init):
    h = pl.program_id(1)
    i = pl.program_id(2)
    slope = slopes_ref[h]
    q = q_ref[...]
    q1, q2 = q[:, :DIFF_QK_DIM], q[:, DIFF_QK_DIM:]
    t0 = i * tq

    def step(start, carry, diag):
        k = k_ref[pl.ds(start, tq), :]
        v = v_ref[pl.ds(start, tq), :]
        k1, k2 = k[:, :DIFF_QK_DIM], k[:, DIFF_QK_DIM:]
        if diag:
            row = lax.broadcasted_iota(jnp.int32, (tq, tq), 0)
            col = lax.broadcasted_iota(jnp.int32, (tq, tq), 1)
            bias = slope * (row - jnp.abs(row - col)).astype(F32)
            mask = (col // CHUNK) <= (row // CHUNK)
        else:
            col = lax.broadcasted_iota(jnp.int32, (1, tq), 1)
            bias = slope * (col + (start - t0)).astype(F32)
        out = []
        for qq, kk, (m, l, acc) in ((q1, k1, carry[0]), (q2, k2, carry[1])):
            s = _dot_nt(qq, kk) + bias
            if diag:
                s = jnp.where(mask, s, NEG_INF)
            out.append(_softmax_step(s, v, m, l, acc))
        return tuple(out)

    def init():
        return (jnp.full((tq, 1), NEG_INF, F32), jnp.zeros((tq, 1), F32), jnp.zeros((tq, HEAD_DIM), F32))

    carry = step(pl.multiple_of(i * tq, tq), (init(), init()), diag=True)

    def body(j, carry):
        return step(pl.multiple_of(j * tq, tq), carry, diag=False)

    (_, l1, acc1), (_, l2, acc2) = lax.fori_loop(0, i, body, carry)
    lam = (jnp.exp(jnp.sum(lq1_ref[...] * lk1_ref[...], axis=-1, keepdims=True))
           - jnp.exp(jnp.sum(lq2_ref[...] * lk2_ref[...], axis=-1, keepdims=True)) + lambda_init)
    o = acc1 * (1.0 / l1) - acc2 * (lam / l2)
    o = _rms_rows(o, g_ref[...], DIFF_EPS) * (1.0 - lambda_init)
    o_ref[...] = o.astype(o_ref.dtype)


def _diff_attention(proj, lq1, lk1, lq2, lk2, subln, lambda_init, *, tq=512):
    b, seq, _ = proj.shape
    slopes = jnp.exp2(-ALIBI_MAX_EXP * jnp.arange(1, N_HEADS + 1, dtype=F32) / N_HEADS)
    vec = pl.BlockSpec((1, DIFF_QK_DIM), lambda b, h, i: (0, 0))
    return pl.pallas_call(
        functools.partial(_diff_kernel, tq=tq, lambda_init=lambda_init),
        grid=(b, N_HEADS, seq // tq),
        in_specs=[pl.BlockSpec(memory_space=pltpu.SMEM)] + _qkv_specs(seq, tq)
        + [vec, vec, vec, vec, pl.BlockSpec((1, HEAD_DIM), lambda b, h, i: (0, 0))],
        out_specs=_out_spec(tq),
        out_shape=jax.ShapeDtypeStruct((b, seq, ATTN_WIDTH), BF16),
        compiler_params=_params(("arbitrary", "arbitrary", "arbitrary")),
        name="diff_attention",
    )(slopes, proj, proj, proj, lq1.reshape(1, -1), lk1.reshape(1, -1), lq2.reshape(1, -1),
      lk2.reshape(1, -1), subln.reshape(1, -1))


def _out_proj_kernel(x_ref, o_ref, w_ref, y_ref):
    y_ref[...] = x_ref[...] + _dot(o_ref[...], w_ref[...])


def _out_proj(x2, o2, w, *, tm=512):
    m, d = x2.shape
    a = o2.shape[1]
    return pl.pallas_call(
        _out_proj_kernel,
        grid=(m // tm,),
        in_specs=[pl.BlockSpec((tm, d), lambda i: (i, 0)),
                  pl.BlockSpec((tm, a), lambda i: (i, 0)),
                  pl.BlockSpec((a, d), lambda i: (0, 0))],
        out_specs=pl.BlockSpec((tm, d), lambda i: (i, 0)),
        out_shape=jax.ShapeDtypeStruct((m, d), F32),
        compiler_params=_params(("arbitrary",)),
        name="out_proj",
    )(x2, o2, w)


def _ffn_kernel(x_ref, g_ref, wg_ref, wu_ref, wd_ref, gf_ref, y_ref, h_ref, *, final_norm):
    f = pl.program_id(1)

    @pl.when(f == 0)
    def _():
        x = x_ref[...]
        h_ref[...] = _rms_rows(x, g_ref[...], RMS_EPS).astype(BF16)
        y_ref[...] = x

    h = h_ref[...]
    gate = _dot(h, wg_ref[...])
    up = _dot(h, wu_ref[...])
    act = (gate * jax.nn.sigmoid(gate) * up).astype(BF16)
    y_ref[...] += _dot(act, wd_ref[...])

    if final_norm:
        @pl.when(f == pl.num_programs(1) - 1)
        def _():
            y_ref[...] = _rms_rows(y_ref[...], gf_ref[...], RMS_EPS)


def _ffn(x2, g, w_gate_up, w_down, g_final, *, final_norm, tm=512, tf=512):
    m, d = x2.shape
    d_ff = w_down.shape[0]
    nf = d_ff // tf
    return pl.pallas_call(
        functools.partial(_ffn_kernel, final_norm=final_norm),
        grid=(m // tm, nf),
        in_specs=[pl.BlockSpec((tm, d), lambda i, f: (i, 0)),
                  pl.BlockSpec((1, d), lambda i, f: (0, 0)),
                  pl.BlockSpec((d, tf), lambda i, f: (0, f)),
                  pl.BlockSpec((d, tf), lambda i, f: (0, f + nf)),
                  pl.BlockSpec((tf, d), lambda i, f: (f, 0)),
                  pl.BlockSpec((1, d), lambda i, f: (0, 0))],
        out_specs=pl.BlockSpec((tm, d), lambda i, f: (i, 0)),
        out_shape=jax.ShapeDtypeStruct((m, d), F32),
        scratch_shapes=[pltpu.VMEM((tm, d), BF16)],
        compiler_params=_params(("arbitrary", "arbitrary")),
        name="ffn_final" if final_norm else "ffn",
    )(x2, g.reshape(1, d), w_gate_up, w_gate_up, w_down, g_final.reshape(1, d))


def _qk_colscale(scale):
    return jnp.concatenate([jnp.full((ATTN_WIDTH,), scale, F32), jnp.ones((2 * ATTN_WIDTH,), F32)])


def kernel(x, norm_mix, fox_w_in, fox_b_f, fox_w_out, sb_w_in, sb_w_out, diff_w_in, diff_lambda_q1, diff_lambda_k1, diff_lambda_q2, diff_lambda_k2, diff_subln, diff_w_out, norm_ffn, ffn_w_gate_up, ffn_w_down, final_norm):
    b, seq, d = x.shape
    depth = norm_mix.shape[0]
    a3 = 3 * ATTN_WIDTH
    x2 = x.reshape(b * seq, d)
    for i in range(depth):
        kind, j = i % N_MIXERS, i // N_MIXERS
        if kind == 0:
            w_in = fox_w_in[j]
            w_gate = jnp.pad(w_in[:, a3:], ((0, 0), (0, LANES - N_HEADS))).astype(BF16)
            proj, gate_logits = _norm_proj(x2, norm_mix[i], w_in[:, :a3].astype(BF16),
                                           _qk_colscale(HEAD_DIM ** -0.5), w_gate)
            b_f_row = jnp.pad(fox_b_f[j].astype(F32), (0, LANES - N_HEADS)).reshape(1, LANES)
            cum = _gate_cumsum(gate_logits, b_f_row, seq)
            cum = cum[:, :N_HEADS].reshape(b, seq, N_HEADS).transpose(0, 2, 1)
            o = _fox_attention(proj.reshape(b, seq, a3), cum)
            w_out = fox_w_out[j]
        elif kind == 1:
            proj = _norm_proj(x2, norm_mix[i], sb_w_in[j].astype(BF16), _qk_colscale(HEAD_DIM ** -0.5))[0]
            o = _sb_attention(proj.reshape(b, seq, a3))
            w_out = sb_w_out[j]
        else:
            lambda_init = 0.8 - 0.6 * float(np.exp(-0.3 * i))
            proj = _norm_proj(x2, norm_mix[i], diff_w_in[j].astype(BF16), _qk_colscale(DIFF_QK_DIM ** -0.5))[0]
            o = _diff_attention(proj.reshape(b, seq, a3), diff_lambda_q1[j], diff_lambda_k1[j],
                                diff_lambda_q2[j], diff_lambda_k2[j], diff_subln[j], lambda_init)
            w_out = diff_w_out[j]
        x2 = _out_proj(x2, o.reshape(b * seq, ATTN_WIDTH), w_out.astype(BF16))
        x2 = _ffn(x2, norm_ffn[i], ffn_w_gate_up[i].astype(BF16), ffn_w_down[i].astype(BF16),
                  final_norm, final_norm=(i == depth - 1))
    return x2.reshape(b, seq, d)
```

```python
import functools
import math

import numpy as np
import jax
import jax.numpy as jnp
from jax import lax
from jax.experimental import pallas as pl
from jax.experimental.pallas import tpu as pltpu

N_MIXERS = 3
N_HEADS = 4
HEAD_DIM = 256
ATTN_WIDTH = N_HEADS * HEAD_DIM
DIFF_QK_DIM = HEAD_DIM // 2
CHUNK = 64
RMS_EPS = 1e-6
DIFF_EPS = 1e-5
ALIBI_MAX_EXP = 8.0
LOG2E = math.log2(math.e)

LANES = 128
VMEM_LIMIT_BYTES = 56 * 1024 * 1024

BF16 = jnp.bfloat16
F32 = jnp.float32
NEG_INF = float("-inf")


def _params(semantics):
    return pltpu.CompilerParams(dimension_semantics=semantics, vmem_limit_bytes=VMEM_LIMIT_BYTES)


def _dot(a, b):
    return jnp.dot(a, b, preferred_element_type=F32)


def _dot_nt(a, b):
    return lax.dot_general(a, b, (((1,), (1,)), ((), ())), preferred_element_type=F32)


def _rms_rows(x, g, eps):
    return x * lax.rsqrt(jnp.mean(x * x, axis=-1, keepdims=True) + eps) * g


def _norm_proj_kernel(x_ref, g_ref, w_ref, cs_ref, *rest, has_gate):
    if has_gate:
        wg_ref, o_ref, og_ref, h_ref = rest
    else:
        o_ref, h_ref = rest

    @pl.when(pl.program_id(1) == 0)
    def _():
        h = _rms_rows(x_ref[...], g_ref[...], RMS_EPS).astype(BF16)
        h_ref[...] = h
        if has_gate:
            og_ref[...] = _dot(h, wg_ref[...])

    o_ref[...] = (_dot(h_ref[...], w_ref[...]) * cs_ref[...]).astype(o_ref.dtype)


def _norm_proj(x2, g, w_all, layer, colscale, w_gate=None, *, tm=512, tn=1024):
    m, d = x2.shape
    n = w_all.shape[2]
    has_gate = w_gate is not None
    in_specs = [
        pl.BlockSpec((tm, d), lambda i, j: (i, 0)),
        pl.BlockSpec((1, d), lambda i, j: (0, 0)),
        pl.BlockSpec((None, d, tn), lambda i, j: (layer, 0, j)),
        pl.BlockSpec((1, tn), lambda i, j: (0, j)),
    ]
    args = [x2, g.reshape(1, d), w_all, colscale.reshape(1, n)]
    out_shape = [jax.ShapeDtypeStruct((m, n), BF16)]
    out_specs = [pl.BlockSpec((tm, tn), lambda i, j: (i, j))]
    if has_gate:
        in_specs.append(pl.BlockSpec((d, LANES), lambda i, j: (0, 0)))
        args.append(w_gate)
        out_shape.append(jax.ShapeDtypeStruct((m, LANES), F32))
        out_specs.append(pl.BlockSpec((tm, LANES), lambda i, j: (i, 0)))
    return pl.pallas_call(
        functools.partial(_norm_proj_kernel, has_gate=has_gate),
        grid=(m // tm, n // tn),
        in_specs=in_specs,
        out_specs=out_specs,
        out_shape=out_shape,
        scratch_shapes=[pltpu.VMEM((tm, d), BF16)],
        compiler_params=_params(("arbitrary", "arbitrary")),
        name="norm_proj_gate" if has_gate else "norm_proj",
    )(*args)


def _split3(x):
    hi = x.astype(BF16)
    r = x - hi.astype(F32)
    mid = r.astype(BF16)
    lo = (r - mid.astype(F32)).astype(BF16)
    return hi, mid, lo


def _log_sigmoid(x):
    return -(jnp.maximum(-x, 0.0) + jnp.log1p(jnp.exp(-jnp.abs(x))))


def _gate_cumsum_kernel(gl_ref, bf_ref, o_ref, carry_ref, *, tiles_per_batch):
    i = pl.program_id(0)

    @pl.when(i % tiles_per_batch == 0)
    def _():
        carry_ref[...] = jnp.zeros_like(carry_ref)

    ls = _log_sigmoid(gl_ref[...] + bf_ref[...])
    tm = ls.shape[0]
    row = lax.broadcasted_iota(jnp.int32, (tm, tm), 0)
    col = lax.broadcasted_iota(jnp.int32, (tm, tm), 1)
    lower = jnp.where(col <= row, 1.0, 0.0).astype(BF16)
    hi, mid, lo = _split3(ls)
    cum = _dot(lower, hi) + _dot(lower, mid) + _dot(lower, lo) + carry_ref[...]
    carry_ref[...] = cum[tm - 1:tm, :]
    o_ref[...] = (cum * LOG2E).T


def _gate_cumsum(gate_logits, b_f_row, seq, *, tm=512):
    m = gate_logits.shape[0]
    return pl.pallas_call(
        functools.partial(_gate_cumsum_kernel, tiles_per_batch=seq // tm),
        grid=(m // tm,),
        in_specs=[pl.BlockSpec((tm, LANES), lambda i: (i, 0)),
                  pl.BlockSpec((1, LANES), lambda i: (0, 0))],
        out_specs=pl.BlockSpec((LANES, tm), lambda i: (0, i)),
        out_shape=jax.ShapeDtypeStruct((LANES, m), F32),
        scratch_shapes=[pltpu.VMEM((1, LANES), F32)],
        compiler_params=_params(("arbitrary",)),
        name="gate_cumsum",
    )(gate_logits, b_f_row)


def _qkv_specs(seq, tq):
    return [
        pl.BlockSpec((None, tq, HEAD_DIM), lambda b, h, i: (b, i, h)),
        pl.BlockSpec((None, seq, HEAD_DIM), lambda b, h, i: (b, 0, N_HEADS + h), pipeline_mode=pl.Buffered(1)),
        pl.BlockSpec((None, seq, HEAD_DIM), lambda b, h, i: (b, 0, 2 * N_HEADS + h), pipeline_mode=pl.Buffered(1)),
    ]


def _out_spec(tq):
    return pl.BlockSpec((None, tq, HEAD_DIM), lambda b, h, i: (b, i, h))


def _lane_blocks(x):
    return [x[:, c * LANES:(c + 1) * LANES] for c in range(x.shape[1] // LANES)]


def _softmax_update(s, v, m_ref, l_ref, acc_ref, rows):
    cols = _lane_blocks(s)
    m_prev = m_ref[rows, :]
    m_new = jnp.maximum(m_prev, jnp.max(functools.reduce(jnp.maximum, cols), axis=-1, keepdims=True))
    alpha = jnp.exp2(m_prev - m_new)
    ps = [jnp.exp2(c - m_new) for c in cols]
    l_ref[rows, :] = alpha * l_ref[rows, :] + functools.reduce(jnp.add, ps)
    m_ref[rows, :] = m_new
    pv = _dot(jnp.concatenate([p.astype(BF16) for p in ps], axis=-1), v)
    for c in range(acc_ref.shape[1] // LANES):
        sl = slice(c * LANES, (c + 1) * LANES)
        acc_ref[rows, sl] = alpha * acc_ref[rows, sl] + pv[:, sl]


def _init_softmax_state(m_ref, l_ref, acc_ref):
    m_ref[...] = jnp.full(m_ref.shape, NEG_INF, F32)
    l_ref[...] = jnp.zeros(l_ref.shape, F32)
    acc_ref[...] = jnp.zeros(acc_ref.shape, F32)


def _normalised(l_ref, acc_ref):
    return acc_ref[...] * (1.0 / jnp.sum(l_ref[...], axis=-1, keepdims=True))


def _two_tile_loop(n_pairs, logits, update, sa_ref, sb_ref, finished=None):
    def store(refs, vals):
        for r, x in zip(refs, vals):
            r[...] = x

    @pl.when(n_pairs > 0)
    def _():
        store(sa_ref, logits(0))

    def pair(j):
        store(sb_ref, logits(2 * j + 1))
        update([r[...] for r in sa_ref], 2 * j)
        store(sa_ref, logits(2 * j + 2))
        update([r[...] for r in sb_ref], 2 * j + 1)

    if finished is None:
        def body(j, carry):
            pair(j)
            return carry
        lax.fori_loop(0, n_pairs, body, 0)
    else:
        def body(carry):
            pair(carry[0])
            return carry[0] + 1, finished()
        lax.while_loop(lambda c: jnp.logical_and(c[0] < n_pairs, jnp.logical_not(c[1])), body,
                       (jnp.int32(0), finished()))


def _fox_kernel(q_ref, k_ref, v_ref, ck_ref, cq_ref, o_ref, sa_ref, sb_ref, m_ref, l_ref, acc_ref,
                *, tq, tk):
    i = pl.program_id(2)
    r = tq // tk
    c0 = cq_ref[:, 0:1]

    def logits(tile, rows=slice(None)):
        start = pl.multiple_of(tile * tk, tk)
        return _dot_nt(q_ref[rows, :], k_ref[pl.ds(start, tk), :]) + (c0 - ck_ref[:, pl.ds(start, tk)])

    def update(s, tile, rows=slice(None)):
        start = pl.multiple_of(tile * tk, tk)
        _softmax_update(s, v_ref[pl.ds(start, tk), :], m_ref, l_ref, acc_ref, rows)

    _init_softmax_state(m_ref, l_ref, acc_ref)
    for d in range(r):
        rows = slice(d * tk, tq)
        n = tq - d * tk
        row = lax.broadcasted_iota(jnp.int32, (n, tk), 0)
        col = lax.broadcasted_iota(jnp.int32, (n, tk), 1)
        update(jnp.where(col <= row, logits(i * r + d, rows), NEG_INF), i * r + d, rows)

    _two_tile_loop(i * r // 2, lambda t: [logits(t)], lambda s, t: update(s[0], t), [sa_ref], [sb_ref])
    o_ref[...] = _normalised(l_ref, acc_ref).astype(o_ref.dtype)


def _softmax_scratch(tq):
    return [pltpu.VMEM((tq, LANES), F32), pltpu.VMEM((tq, LANES), F32), pltpu.VMEM((tq, HEAD_DIM), F32)]


def _fox_attention(proj, cum_t, *, tq=1024, tk=512):
    b, seq, _ = proj.shape
    assert tq % (2 * tk) == 0 and seq % tq == 0
    cum3 = cum_t.reshape(LANES, 1, b * seq)
    return pl.pallas_call(
        functools.partial(_fox_kernel, tq=tq, tk=tk),
        grid=(b, N_HEADS, seq // tq),
        in_specs=_qkv_specs(seq, tq) + [
            pl.BlockSpec((None, 1, seq), lambda b, h, i: (h, 0, b)),
            pl.BlockSpec((None, 1, tq), lambda b, h, i, n=seq // tq: (h, 0, b * n + i)),
        ],
        out_specs=_out_spec(tq),
        out_shape=jax.ShapeDtypeStruct((b, seq, ATTN_WIDTH), BF16),
        scratch_shapes=[pltpu.VMEM((tq, tk), F32), pltpu.VMEM((tq, tk), F32)] + _softmax_scratch(tq),
        compiler_params=_params(("arbitrary", "arbitrary", "arbitrary")),
        name="fox_attention",
    )(proj, proj, proj, cum3, cum3)


def _split2(x):
    hi = x.astype(BF16)
    lo = (x - hi.astype(F32)).astype(BF16)
    return hi, lo


SB_DEAD_LOG2 = 160.0


def _sb_kernel(q_ref, k_ref, v_ref, o_ref, za_ref, zb_ref, tail_ref, acc_ref, *, tq, tk):
    i = pl.program_id(2)
    r = tq // tk
    row_k = lax.broadcasted_iota(jnp.int32, (2 * tk, tk), 0) & (tk - 1)
    col_k = lax.broadcasted_iota(jnp.int32, (2 * tk, tk), 1)
    suffix = jnp.where(row_k >= col_k, 1.0, 0.0).astype(BF16)

    def logits(tile, rows=slice(None)):
        start = pl.multiple_of(tile * tk, tk)
        return _dot_nt(q_ref[rows, :], k_ref[pl.ds(start, tk), :])

    def update(z, tile, rows=slice(None), masked=False):
        start = pl.multiple_of(tile * tk, tk)
        neg_abs = lax.bitcast_convert_type(
            lax.bitcast_convert_type(z, jnp.uint32) | jnp.uint32(0x80000000), F32)
        sp = jnp.maximum(z, 0.0) + jnp.log2(1.0 + jnp.exp2(neg_abs))
        if masked:
            n = z.shape[0]
            mask = lax.broadcasted_iota(jnp.int32, (n, tk), 1) < lax.broadcasted_iota(jnp.int32, (n, tk), 0)
            sp = jnp.where(mask, sp, 0.0)
        incl = _dot(jnp.concatenate(_split2(sp), axis=-1), suffix)
        tail = tail_ref[rows, :]
        a = jnp.concatenate([jnp.exp2(zc - ic - tail) for zc, ic in zip(_lane_blocks(z), _lane_blocks(incl))],
                            axis=-1)
        if masked:
            a = jnp.where(mask, a, 0.0)
        acc_ref[rows, :] += _dot(a.astype(BF16), v_ref[pl.ds(start, tk), :])
        tail_ref[rows, :] = tail + jnp.sum(functools.reduce(jnp.add, _lane_blocks(sp)), axis=-1, keepdims=True)

    tail_ref[...] = jnp.zeros(tail_ref.shape, F32)
    acc_ref[...] = jnp.zeros(acc_ref.shape, F32)
    for d in reversed(range(r)):
        rows = slice(d * tk, tq)
        update(logits(i * r + d, rows), i * r + d, rows, masked=True)

    n_off = i * r
    _two_tile_loop(n_off // 2, lambda t: [logits(jnp.maximum(n_off - 1 - t, 0))],
                   lambda z, t: update(z[0], n_off - 1 - t), [za_ref], [zb_ref],
                   finished=lambda: jnp.min(tail_ref[...]) >= SB_DEAD_LOG2)
    o_ref[...] = acc_ref[...].astype(o_ref.dtype)


def _sb_attention(proj, *, tq=1024, tk=256):
    b, seq, _ = proj.shape
    assert tq % (2 * tk) == 0 and seq % tq == 0
    return pl.pallas_call(
        functools.partial(_sb_kernel, tq=tq, tk=tk),
        grid=(b, N_HEADS, seq // tq),
        in_specs=_qkv_specs(seq, tq),
        out_specs=_out_spec(tq),
        out_shape=jax.ShapeDtypeStruct((b, seq, ATTN_WIDTH), BF16),
        scratch_shapes=[pltpu.VMEM((tq, tk), F32), pltpu.VMEM((tq, tk), F32),
                        pltpu.VMEM((tq, LANES), F32), pltpu.VMEM((tq, HEAD_DIM), F32)],
        compiler_params=_params(("arbitrary", "arbitrary", "arbitrary")),
        name="sb_attention",
    )(proj, proj, proj)


def _diff_kernel(slopes_ref, q_ref, k_ref, v_ref, lq1_ref, lk1_ref, lq2_ref, lk2_ref, g_ref, o_ref,
                 sa_ref, sb_ref, m_ref, l_ref, acc_ref, o1_ref, *, tq, tk, lambda_init):
    h = pl.program_id(1)
    i = pl.program_id(2)
    r = tq // tk
    slope = slopes_ref[h]

    def attend(hs):
        def product(tile, rows=slice(None)):
            start = pl.multiple_of(tile * tk, tk)
            return _dot_nt(q_ref[rows, hs], k_ref[pl.ds(start, tk), hs])

        def logits(tile):
            col = lax.broadcasted_iota(jnp.int32, (1, tk), 1)
            return product(tile) + slope * (col + (tile * tk - i * tq)).astype(F32)

        def update(s, tile, rows=slice(None)):
            start = pl.multiple_of(tile * tk, tk)
            _softmax_update(s, v_ref[pl.ds(start, tk), :], m_ref, l_ref, acc_ref, rows)

        _init_softmax_state(m_ref, l_ref, acc_ref)
        for d in range(r):
            rows = slice(d * tk, tq)
            n = tq - d * tk
            row = lax.broadcasted_iota(jnp.int32, (n, tk), 0)
            col = lax.broadcasted_iota(jnp.int32, (n, tk), 1)
            bias = slope * (row + d * tk - jnp.abs(row - col)).astype(F32)
            mask = (col >> 6) <= (row >> 6)
            update(jnp.where(mask, product(i * r + d, rows) + bias, NEG_INF), i * r + d, rows)
        _two_tile_loop(i * r // 2, lambda t: [logits(t)], lambda s, t: update(s[0], t), [sa_ref], [sb_ref])
        return _normalised(l_ref, acc_ref)

    o1_ref[...] = attend(slice(0, DIFF_QK_DIM))
    o2 = attend(slice(DIFF_QK_DIM, HEAD_DIM))
    lam = (jnp.exp(jnp.sum(lq1_ref[...] * lk1_ref[...], axis=-1, keepdims=True))
           - jnp.exp(jnp.sum(lq2_ref[...] * lk2_ref[...], axis=-1, keepdims=True)) + lambda_init)
    o = o1_ref[...] - lam * o2
    o = _rms_rows(o, g_ref[...], DIFF_EPS) * (1.0 - lambda_init)
    o_ref[...] = o.astype(o_ref.dtype)


def _diff_attention(proj, lq1, lk1, lq2, lk2, subln, lambda_init, *, tq=1024, tk=512):
    b, seq, _ = proj.shape
    assert CHUNK == 64 and tk % CHUNK == 0 and tq % (2 * tk) == 0 and seq % tq == 0
    slopes = LOG2E * jnp.exp2(-ALIBI_MAX_EXP * jnp.arange(1, N_HEADS + 1, dtype=F32) / N_HEADS)
    vec = pl.BlockSpec((1, DIFF_QK_DIM), lambda b, h, i: (0, 0))
    return pl.pallas_call(
        functools.partial(_diff_kernel, tq=tq, tk=tk, lambda_init=lambda_init),
        grid=(b, N_HEADS, seq // tq),
        in_specs=[pl.BlockSpec(memory_space=pltpu.SMEM)] + _qkv_specs(seq, tq)
        + [vec, vec, vec, vec, pl.BlockSpec((1, HEAD_DIM), lambda b, h, i: (0, 0))],
        out_specs=_out_spec(tq),
        out_shape=jax.ShapeDtypeStruct((b, seq, ATTN_WIDTH), BF16),
        scratch_shapes=[pltpu.VMEM((tq, tk), F32), pltpu.VMEM((tq, tk), F32)] + _softmax_scratch(tq)
        + [pltpu.VMEM((tq, HEAD_DIM), F32)],
        compiler_params=_params(("arbitrary", "arbitrary", "arbitrary")),
        name="diff_attention",
    )(slopes, proj, proj, proj, lq1.reshape(1, -1), lk1.reshape(1, -1), lq2.reshape(1, -1),
      lk2.reshape(1, -1), subln.reshape(1, -1))


def _out_proj_kernel(x_ref, o_ref, w_ref, y_ref):
    y_ref[...] = x_ref[...] + _dot(o_ref[...], w_ref[...])


def _out_proj(x2, o2, w_all, layer, *, tm=512):
    m, d = x2.shape
    a = o2.shape[1]
    return pl.pallas_call(
        _out_proj_kernel,
        grid=(m // tm,),
        in_specs=[pl.BlockSpec((tm, d), lambda i: (i, 0)),
                  pl.BlockSpec((tm, a), lambda i: (i, 0)),
                  pl.BlockSpec((None, a, d), lambda i: (layer, 0, 0))],
        out_specs=pl.BlockSpec((tm, d), lambda i: (i, 0)),
        out_shape=jax.ShapeDtypeStruct((m, d), F32),
        compiler_params=_params(("arbitrary",)),
        name="out_proj",
    )(x2, o2, w_all)


def _ffn_kernel(x_ref, g_ref, wg_ref, wu_ref, wd_ref, gf_ref, y_ref, h_ref, *, final_norm):
    f = pl.program_id(1)

    @pl.when(f == 0)
    def _():
        x = x_ref[...]
        h_ref[...] = _rms_rows(x, g_ref[...], RMS_EPS).astype(BF16)
        y_ref[...] = x

    h = h_ref[...]
    gate = _dot(h, wg_ref[...])
    up = _dot(h, wu_ref[...])
    act = (gate * jax.nn.sigmoid(gate) * up).astype(BF16)
    y_ref[...] += _dot(act, wd_ref[...])

    if final_norm:
        @pl.when(f == pl.num_programs(1) - 1)
        def _():
            y_ref[...] = _rms_rows(y_ref[...], gf_ref[...], RMS_EPS)


def _ffn(x2, g, w_gate_up, w_down, layer, g_final, *, final_norm, tm=512, tf=512):
    m, d = x2.shape
    d_ff = w_down.shape[1]
    nf = d_ff // tf
    return pl.pallas_call(
        functools.partial(_ffn_kernel, final_norm=final_norm),
        grid=(m // tm, nf),
        in_specs=[pl.BlockSpec((tm, d), lambda i, f: (i, 0)),
                  pl.BlockSpec((1, d), lambda i, f: (0, 0)),
                  pl.BlockSpec((None, d, tf), lambda i, f: (layer, 0, f)),
                  pl.BlockSpec((None, d, tf), lambda i, f: (layer, 0, f + nf)),
                  pl.BlockSpec((None, tf, d), lambda i, f: (layer, f, 0)),
                  pl.BlockSpec((1, d), lambda i, f: (0, 0))],
        out_specs=pl.BlockSpec((tm, d), lambda i, f: (i, 0)),
        out_shape=jax.ShapeDtypeStruct((m, d), F32),
        scratch_shapes=[pltpu.VMEM((tm, d), BF16)],
        compiler_params=_params(("arbitrary", "arbitrary")),
        name="ffn_final" if final_norm else "ffn",
    )(x2, g.reshape(1, d), w_gate_up, w_gate_up, w_down, g_final.reshape(1, d))


def _qk_colscale(scale):
    return jnp.concatenate([jnp.full((ATTN_WIDTH,), scale * LOG2E, F32), jnp.ones((2 * ATTN_WIDTH,), F32)])


def kernel(x, norm_mix, fox_w_in, fox_b_f, fox_w_out, sb_w_in, sb_w_out, diff_w_in, diff_lambda_q1, diff_lambda_k1, diff_lambda_q2, diff_lambda_k2, diff_subln, diff_w_out, norm_ffn, ffn_w_gate_up, ffn_w_down, final_norm):
    b, seq, d = x.shape
    depth = norm_mix.shape[0]
    a3 = 3 * ATTN_WIDTH
    fox_w = fox_w_in[:, :, :a3].astype(BF16)
    fox_w_gate = jnp.pad(fox_w_in[:, :, a3:], ((0, 0), (0, 0), (0, LANES - N_HEADS))).astype(BF16)
    sb_w, diff_w = sb_w_in.astype(BF16), diff_w_in.astype(BF16)
    w_out = {0: fox_w_out.astype(BF16), 1: sb_w_out.astype(BF16), 2: diff_w_out.astype(BF16)}
    w_gate_up, w_down = ffn_w_gate_up.astype(BF16), ffn_w_down.astype(BF16)
    x2 = x.reshape(b * seq, d)
    for i in range(depth):
        kind, j = i % N_MIXERS, i // N_MIXERS
        if kind == 0:
            proj, gate_logits = _norm_proj(x2, norm_mix[i], fox_w, j, _qk_colscale(HEAD_DIM ** -0.5),
                                           fox_w_gate[j])
            b_f_row = jnp.pad(fox_b_f[j].astype(F32), (0, LANES - N_HEADS)).reshape(1, LANES)
            cum_t = _gate_cumsum(gate_logits, b_f_row, seq)
            o = _fox_attention(proj.reshape(b, seq, a3), cum_t)
        elif kind == 1:
            proj = _norm_proj(x2, norm_mix[i], sb_w, j, _qk_colscale(HEAD_DIM ** -0.5))[0]
            o = _sb_attention(proj.reshape(b, seq, a3))
        else:
            lambda_init = 0.8 - 0.6 * float(np.exp(-0.3 * i))
            proj = _norm_proj(x2, norm_mix[i], diff_w, j, _qk_colscale(DIFF_QK_DIM ** -0.5))[0]
            o = _diff_attention(proj.reshape(b, seq, a3), diff_lambda_q1[j], diff_lambda_k1[j],
                                diff_lambda_q2[j], diff_lambda_k2[j], diff_subln[j], lambda_init)
        x2 = _out_proj(x2, o.reshape(b * seq, ATTN_WIDTH), w_out[kind], j)
        x2 = _ffn(x2, norm_ffn[i], w_gate_up, w_down, i, final_norm, final_norm=(i == depth - 1))
    return x2.reshape(b, seq, d)
```

```python
import functools
import math

import numpy as np
import jax
import jax.numpy as jnp
from jax import lax
from jax.experimental import pallas as pl
from jax.experimental.pallas import tpu as pltpu

N_MIXERS = 3
N_HEADS = 4
HEAD_DIM = 256
ATTN_WIDTH = N_HEADS * HEAD_DIM
DIFF_QK_DIM = HEAD_DIM // 2
CHUNK = 64
RMS_EPS = 1e-6
DIFF_EPS = 1e-5
ALIBI_MAX_EXP = 8.0
LOG2E = math.log2(math.e)

LANES = 128
VMEM_LIMIT_BYTES = 56 * 1024 * 1024

BF16 = jnp.bfloat16
F32 = jnp.float32
NEG_INF = float("-inf")


def _params(semantics):
    return pltpu.CompilerParams(dimension_semantics=semantics, vmem_limit_bytes=VMEM_LIMIT_BYTES)


def _dot(a, b):
    return jnp.dot(a, b, preferred_element_type=F32)


def _dot_nt(a, b):
    return lax.dot_general(a, b, (((1,), (1,)), ((), ())), preferred_element_type=F32)


def _rms_rows(x, g, eps):
    return x * lax.rsqrt(jnp.mean(x * x, axis=-1, keepdims=True) + eps) * g


def _norm_proj_kernel(x_ref, g_ref, w_ref, cs_ref, *rest, has_gate):
    if has_gate:
        wg_ref, o_ref, og_ref, h_ref = rest
    else:
        o_ref, h_ref = rest

    @pl.when(pl.program_id(1) == 0)
    def _():
        h = _rms_rows(x_ref[...], g_ref[...], RMS_EPS).astype(BF16)
        h_ref[...] = h
        if has_gate:
            og_ref[...] = _dot(h, wg_ref[...])

    o_ref[...] = (_dot(h_ref[...], w_ref[...]) * cs_ref[...]).astype(o_ref.dtype)


def _norm_proj(x2, g, w_all, layer, colscale, w_gate=None, *, tm=512, tn=1024):
    m, d = x2.shape
    n = w_all.shape[2]
    has_gate = w_gate is not None
    in_specs = [
        pl.BlockSpec((tm, d), lambda i, j: (i, 0)),
        pl.BlockSpec((1, d), lambda i, j: (0, 0)),
        pl.BlockSpec((None, d, tn), lambda i, j: (layer, 0, j)),
        pl.BlockSpec((1, tn), lambda i, j: (0, j)),
    ]
    args = [x2, g.reshape(1, d), w_all, colscale.reshape(1, n)]
    out_shape = [jax.ShapeDtypeStruct((m, n), BF16)]
    out_specs = [pl.BlockSpec((tm, tn), lambda i, j: (i, j))]
    if has_gate:
        in_specs.append(pl.BlockSpec((d, LANES), lambda i, j: (0, 0)))
        args.append(w_gate)
        out_shape.append(jax.ShapeDtypeStruct((m, LANES), F32))
        out_specs.append(pl.BlockSpec((tm, LANES), lambda i, j: (i, 0)))
    return pl.pallas_call(
        functools.partial(_norm_proj_kernel, has_gate=has_gate),
        grid=(m // tm, n // tn),
        in_specs=in_specs,
        out_specs=out_specs,
        out_shape=out_shape,
        scratch_shapes=[pltpu.VMEM((tm, d), BF16)],
        compiler_params=_params(("arbitrary", "arbitrary")),
        name="norm_proj_gate" if has_gate else "norm_proj",
    )(*args)


def _split3(x):
    hi = x.astype(BF16)
    r = x - hi.astype(F32)
    mid = r.astype(BF16)
    lo = (r - mid.astype(F32)).astype(BF16)
    return hi, mid, lo


def _log_sigmoid(x):
    return -(jnp.maximum(-x, 0.0) + jnp.log1p(jnp.exp(-jnp.abs(x))))


def _gate_cumsum_kernel(gl_ref, bf_ref, o_ref, carry_ref, *, tiles_per_batch):
    i = pl.program_id(0)

    @pl.when(i % tiles_per_batch == 0)
    def _():
        carry_ref[...] = jnp.zeros_like(carry_ref)

    ls = _log_sigmoid(gl_ref[...] + bf_ref[...])
    tm = ls.shape[0]
    row = lax.broadcasted_iota(jnp.int32, (tm, tm), 0)
    col = lax.broadcasted_iota(jnp.int32, (tm, tm), 1)
    lower = jnp.where(col <= row, 1.0, 0.0).astype(BF16)
    hi, mid, lo = _split3(ls)
    cum = _dot(lower, hi) + _dot(lower, mid) + _dot(lower, lo) + carry_ref[...]
    carry_ref[...] = cum[tm - 1:tm, :]
    o_ref[...] = (cum * LOG2E).T


def _gate_cumsum(gate_logits, b_f_row, seq, *, tm=512):
    m = gate_logits.shape[0]
    return pl.pallas_call(
        functools.partial(_gate_cumsum_kernel, tiles_per_batch=seq // tm),
        grid=(m // tm,),
        in_specs=[pl.BlockSpec((tm, LANES), lambda i: (i, 0)),
                  pl.BlockSpec((1, LANES), lambda i: (0, 0))],
        out_specs=pl.BlockSpec((LANES, tm), lambda i: (0, i)),
        out_shape=jax.ShapeDtypeStruct((LANES, m), F32),
        scratch_shapes=[pltpu.VMEM((1, LANES), F32)],
        compiler_params=_params(("arbitrary",)),
        name="gate_cumsum",
    )(gate_logits, b_f_row)


def _qkv_specs(seq, tq):
    return [
        pl.BlockSpec((None, tq, HEAD_DIM), lambda b, h, i: (b, i, h)),
        pl.BlockSpec((None, seq, HEAD_DIM), lambda b, h, i: (b, 0, N_HEADS + h), pipeline_mode=pl.Buffered(1)),
        pl.BlockSpec((None, seq, HEAD_DIM), lambda b, h, i: (b, 0, 2 * N_HEADS + h), pipeline_mode=pl.Buffered(1)),
    ]


def _out_spec(tq):
    return pl.BlockSpec((None, tq, HEAD_DIM), lambda b, h, i: (b, i, h))


def _lane_blocks(x):
    return [x[:, c * LANES:(c + 1) * LANES] for c in range(x.shape[1] // LANES)]


def _softmax_update(s, v, m_ref, l_ref, acc_ref, rows):
    cols = _lane_blocks(s)
    m_prev = m_ref[rows, :]
    m_new = jnp.maximum(m_prev, jnp.max(functools.reduce(jnp.maximum, cols), axis=-1, keepdims=True))
    alpha = jnp.exp2(m_prev - m_new)
    ps = [jnp.exp2(c - m_new) for c in cols]
    l_ref[rows, :] = alpha * l_ref[rows, :] + functools.reduce(jnp.add, ps)
    m_ref[rows, :] = m_new
    pv = _dot(jnp.concatenate([p.astype(BF16) for p in ps], axis=-1), v)
    for c in range(acc_ref.shape[1] // LANES):
        sl = slice(c * LANES, (c + 1) * LANES)
        acc_ref[rows, sl] = alpha * acc_ref[rows, sl] + pv[:, sl]


def _init_softmax_state(m_ref, l_ref, acc_ref):
    m_ref[...] = jnp.full(m_ref.shape, NEG_INF, F32)
    l_ref[...] = jnp.zeros(l_ref.shape, F32)
    acc_ref[...] = jnp.zeros(acc_ref.shape, F32)


def _normalised(l_ref, acc_ref):
    return acc_ref[...] * (1.0 / jnp.sum(l_ref[...], axis=-1, keepdims=True))


def _two_tile_loop(n_pairs, logits, update, sa_ref, sb_ref, finished=None):
    def store(refs, vals):
        for r, x in zip(refs, vals):
            r[...] = x

    @pl.when(n_pairs > 0)
    def _():
        store(sa_ref, logits(0))

    def pair(j):
        store(sb_ref, logits(2 * j + 1))
        update([r[...] for r in sa_ref], 2 * j)
        store(sa_ref, logits(2 * j + 2))
        update([r[...] for r in sb_ref], 2 * j + 1)

    if finished is None:
        def body(j, carry):
            pair(j)
            return carry
        lax.fori_loop(0, n_pairs, body, 0)
    else:
        def body(carry):
            pair(carry[0])
            return carry[0] + 1, finished(carry[0] + 1)
        lax.while_loop(lambda c: jnp.logical_and(c[0] < n_pairs, jnp.logical_not(c[1])), body,
                       (jnp.int32(0), finished(jnp.int32(0))))


DEAD_LOG2 = 160.0


def _max_row_norm(x):
    xf = x.astype(F32)
    return jnp.sqrt(jnp.max(jnp.sum(xf * xf, axis=-1, keepdims=True)))


def _key_norm_prefix_max(k_ref, cols, kmax_ref, tk):
    def body(j, running):
        start = pl.multiple_of(j * tk, tk)
        running = jnp.maximum(running, _max_row_norm(k_ref[pl.ds(start, tk), cols]))
        kmax_ref[j] = running
        return running
    lax.fori_loop(0, kmax_ref.shape[0], body, jnp.float32(0.0))


def _fox_kernel(q_ref, k_ref, v_ref, ck_ref, cq_ref, o_ref, sa_ref, sb_ref, m_ref, l_ref, acc_ref,
                kmax_ref, cmin_ref, *, tq, tk):
    i = pl.program_id(2)
    r = tq // tk
    c0 = cq_ref[:, 0:1]

    @pl.when(i == 0)
    def _():
        _key_norm_prefix_max(k_ref, slice(None), kmax_ref, tk)

        def body(j, running):
            start = pl.multiple_of(j * tk, tk)
            running = jnp.minimum(running, jnp.min(ck_ref[:, pl.ds(start, tk)]))
            cmin_ref[j] = running
            return running
        lax.fori_loop(0, cmin_ref.shape[0], body, jnp.float32(jnp.inf))

    def logits(tile, rows=slice(None)):
        start = pl.multiple_of(tile * tk, tk)
        return _dot_nt(q_ref[rows, :], k_ref[pl.ds(start, tk), :]) + (c0 - ck_ref[:, pl.ds(start, tk)])

    def update(s, tile, rows=slice(None)):
        start = pl.multiple_of(tile * tk, tk)
        _softmax_update(s, v_ref[pl.ds(start, tk), :], m_ref, l_ref, acc_ref, rows)

    _init_softmax_state(m_ref, l_ref, acc_ref)
    for d in range(r):
        rows = slice(d * tk, tq)
        n = tq - d * tk
        row = lax.broadcasted_iota(jnp.int32, (n, tk), 0)
        col = lax.broadcasted_iota(jnp.int32, (n, tk), 1)
        update(jnp.where(col <= row, logits(i * r + d, rows), NEG_INF), i * r + d, rows)

    n_off = i * r
    q_norm = _max_row_norm(q_ref[...])
    c0_scalar = jnp.max(c0)

    def finished(trip):
        tile = jnp.maximum(n_off - 1 - 2 * trip, 0)
        bound = q_norm * kmax_ref[tile] + (c0_scalar - cmin_ref[tile])
        return bound - jnp.min(m_ref[...]) <= -DEAD_LOG2

    _two_tile_loop(n_off // 2, lambda t: [logits(jnp.maximum(n_off - 1 - t, 0))],
                   lambda s, t: update(s[0], n_off - 1 - t), [sa_ref], [sb_ref], finished=finished)
    o_ref[...] = _normalised(l_ref, acc_ref).astype(o_ref.dtype)


def _softmax_scratch(tq):
    return [pltpu.VMEM((tq, LANES), F32), pltpu.VMEM((tq, LANES), F32), pltpu.VMEM((tq, HEAD_DIM), F32)]


def _fox_attention(proj, cum_t, *, tq=1024, tk=512):
    b, seq, _ = proj.shape
    assert tq % (2 * tk) == 0 and seq % tq == 0
    cum3 = cum_t.reshape(LANES, 1, b * seq)
    return pl.pallas_call(
        functools.partial(_fox_kernel, tq=tq, tk=tk),
        grid=(b, N_HEADS, seq // tq),
        in_specs=_qkv_specs(seq, tq) + [
            pl.BlockSpec((None, 1, seq), lambda b, h, i: (h, 0, b)),
            pl.BlockSpec((None, 1, tq), lambda b, h, i, n=seq // tq: (h, 0, b * n + i)),
        ],
        out_specs=_out_spec(tq),
        out_shape=jax.ShapeDtypeStruct((b, seq, ATTN_WIDTH), BF16),
        scratch_shapes=[pltpu.VMEM((tq, tk), F32), pltpu.VMEM((tq, tk), F32)] + _softmax_scratch(tq)
        + [pltpu.SMEM((seq // tk,), F32), pltpu.SMEM((seq // tk,), F32)],
        compiler_params=_params(("arbitrary", "arbitrary", "arbitrary")),
        name="fox_attention",
    )(proj, proj, proj, cum3, cum3)


def _split2(x):
    hi = x.astype(BF16)
    lo = (x - hi.astype(F32)).astype(BF16)
    return hi, lo


def _sb_kernel(q_ref, k_ref, v_ref, o_ref, za_ref, zb_ref, tail_ref, acc_ref, *, tq, tk):
    i = pl.program_id(2)
    r = tq // tk
    row_k = lax.broadcasted_iota(jnp.int32, (2 * tk, tk), 0) & (tk - 1)
    col_k = lax.broadcasted_iota(jnp.int32, (2 * tk, tk), 1)
    suffix = jnp.where(row_k >= col_k, 1.0, 0.0).astype(BF16)

    def logits(tile, rows=slice(None)):
        start = pl.multiple_of(tile * tk, tk)
        return _dot_nt(q_ref[rows, :], k_ref[pl.ds(start, tk), :])

    def update(z, tile, rows=slice(None), masked=False):
        start = pl.multiple_of(tile * tk, tk)
        neg_abs = lax.bitcast_convert_type(
            lax.bitcast_convert_type(z, jnp.uint32) | jnp.uint32(0x80000000), F32)
        sp = jnp.maximum(z, 0.0) + jnp.log2(1.0 + jnp.exp2(neg_abs))
        if masked:
            n = z.shape[0]
            mask = lax.broadcasted_iota(jnp.int32, (n, tk), 1) < lax.broadcasted_iota(jnp.int32, (n, tk), 0)
            sp = jnp.where(mask, sp, 0.0)
        incl = _dot(jnp.concatenate(_split2(sp), axis=-1), suffix)
        tail = tail_ref[rows, :]
        a = jnp.concatenate([jnp.exp2(zc - ic - tail) for zc, ic in zip(_lane_blocks(z), _lane_blocks(incl))],
                            axis=-1)
        if masked:
            a = jnp.where(mask, a, 0.0)
        acc_ref[rows, :] += _dot(a.astype(BF16), v_ref[pl.ds(start, tk), :])
        tail_ref[rows, :] = tail + jnp.sum(functools.reduce(jnp.add, _lane_blocks(sp)), axis=-1, keepdims=True)

    tail_ref[...] = jnp.zeros(tail_ref.shape, F32)
    acc_ref[...] = jnp.zeros(acc_ref.shape, F32)
    for d in reversed(range(r)):
        rows = slice(d * tk, tq)
        update(logits(i * r + d, rows), i * r + d, rows, masked=True)

    n_off = i * r
    _two_tile_loop(n_off // 2, lambda t: [logits(jnp.maximum(n_off - 1 - t, 0))],
                   lambda z, t: update(z[0], n_off - 1 - t), [za_ref], [zb_ref],
                   finished=lambda trip: jnp.min(tail_ref[...]) >= DEAD_LOG2)
    o_ref[...] = acc_ref[...].astype(o_ref.dtype)


def _sb_attention(proj, *, tq=1024, tk=256):
    b, seq, _ = proj.shape
    assert tq % (2 * tk) == 0 and seq % tq == 0
    return pl.pallas_call(
        functools.partial(_sb_kernel, tq=tq, tk=tk),
        grid=(b, N_HEADS, seq // tq),
        in_specs=_qkv_specs(seq, tq),
        out_specs=_out_spec(tq),
        out_shape=jax.ShapeDtypeStruct((b, seq, ATTN_WIDTH), BF16),
        scratch_shapes=[pltpu.VMEM((tq, tk), F32), pltpu.VMEM((tq, tk), F32),
                        pltpu.VMEM((tq, LANES), F32), pltpu.VMEM((tq, HEAD_DIM), F32)],
        compiler_params=_params(("arbitrary", "arbitrary", "arbitrary")),
        name="sb_attention",
    )(proj, proj, proj)


def _diff_kernel(slopes_ref, q_ref, k_ref, v_ref, lq1_ref, lk1_ref, lq2_ref, lk2_ref, g_ref, o_ref,
                 sa_ref, sb_ref, m_ref, l_ref, acc_ref, o1_ref, kmax1_ref, kmax2_ref, *, tq, tk, lambda_init):
    h = pl.program_id(1)
    i = pl.program_id(2)
    r = tq // tk
    slope = slopes_ref[h]
    n_off = i * r

    def attend(hs, kmax_ref):
        @pl.when(i == 0)
        def _():
            _key_norm_prefix_max(k_ref, hs, kmax_ref, tk)

        def product(tile, rows=slice(None)):
            start = pl.multiple_of(tile * tk, tk)
            return _dot_nt(q_ref[rows, hs], k_ref[pl.ds(start, tk), hs])

        def logits(tile):
            col = lax.broadcasted_iota(jnp.int32, (1, tk), 1)
            return product(tile) + slope * (col + (tile * tk - i * tq)).astype(F32)

        def update(s, tile, rows=slice(None)):
            start = pl.multiple_of(tile * tk, tk)
            _softmax_update(s, v_ref[pl.ds(start, tk), :], m_ref, l_ref, acc_ref, rows)

        _init_softmax_state(m_ref, l_ref, acc_ref)
        for d in range(r):
            rows = slice(d * tk, tq)
            n = tq - d * tk
            row = lax.broadcasted_iota(jnp.int32, (n, tk), 0)
            col = lax.broadcasted_iota(jnp.int32, (n, tk), 1)
            bias = slope * (row + d * tk - jnp.abs(row - col)).astype(F32)
            mask = (col >> 6) <= (row >> 6)
            update(jnp.where(mask, product(i * r + d, rows) + bias, NEG_INF), i * r + d, rows)

        q_norm = _max_row_norm(q_ref[:, hs])

        def finished(trip):
            tile = jnp.maximum(n_off - 1 - 2 * trip, 0)
            bound = q_norm * kmax_ref[tile] + slope * ((tile + 1) * tk - 1 - i * tq).astype(F32)
            return bound - jnp.min(m_ref[...]) <= -DEAD_LOG2

        _two_tile_loop(n_off // 2, lambda t: [logits(jnp.maximum(n_off - 1 - t, 0))],
                       lambda s, t: update(s[0], n_off - 1 - t), [sa_ref], [sb_ref], finished=finished)
        return _normalised(l_ref, acc_ref)

    o1_ref[...] = attend(slice(0, DIFF_QK_DIM), kmax1_ref)
    o2 = attend(slice(DIFF_QK_DIM, HEAD_DIM), kmax2_ref)
    lam = (jnp.exp(jnp.sum(lq1_ref[...] * lk1_ref[...], axis=-1, keepdims=True))
           - jnp.exp(jnp.sum(lq2_ref[...] * lk2_ref[...], axis=-1, keepdims=True)) + lambda_init)
    o = o1_ref[...] - lam * o2
    o = _rms_rows(o, g_ref[...], DIFF_EPS) * (1.0 - lambda_init)
    o_ref[...] = o.astype(o_ref.dtype)


def _diff_attention(proj, lq1, lk1, lq2, lk2, subln, lambda_init, *, tq=1024, tk=512):
    b, seq, _ = proj.shape
    assert CHUNK == 64 and tk % CHUNK == 0 and tq % (2 * tk) == 0 and seq % tq == 0
    slopes = LOG2E * jnp.exp2(-ALIBI_MAX_EXP * jnp.arange(1, N_HEADS + 1, dtype=F32) / N_HEADS)
    vec = pl.BlockSpec((1, DIFF_QK_DIM), lambda b, h, i: (0, 0))
    return pl.pallas_call(
        functools.partial(_diff_kernel, tq=tq, tk=tk, lambda_init=lambda_init),
        grid=(b, N_HEADS, seq // tq),
        in_specs=[pl.BlockSpec(memory_space=pltpu.SMEM)] + _qkv_specs(seq, tq)
        + [vec, vec, vec, vec, pl.BlockSpec((1, HEAD_DIM), lambda b, h, i: (0, 0))],
        out_specs=_out_spec(tq),
        out_shape=jax.ShapeDtypeStruct((b, seq, ATTN_WIDTH), BF16),
        scratch_shapes=[pltpu.VMEM((tq, tk), F32), pltpu.VMEM((tq, tk), F32)] + _softmax_scratch(tq)
        + [pltpu.VMEM((tq, HEAD_DIM), F32), pltpu.SMEM((seq // tk,), F32), pltpu.SMEM((seq // tk,), F32)],
        compiler_params=_params(("arbitrary", "arbitrary", "arbitrary")),
        name="diff_attention",
    )(slopes, proj, proj, proj, lq1.reshape(1, -1), lk1.reshape(1, -1), lq2.reshape(1, -1),
      lk2.reshape(1, -1), subln.reshape(1, -1))


def _out_proj_kernel(x_ref, o_ref, w_ref, y_ref):
    y_ref[...] = x_ref[...] + _dot(o_ref[...], w_ref[...])


def _out_proj(x2, o2, w_all, layer, *, tm=512):
    m, d = x2.shape
    a = o2.shape[1]
    return pl.pallas_call(
        _out_proj_kernel,
        grid=(m // tm,),
        in_specs=[pl.BlockSpec((tm, d), lambda i: (i, 0)),
                  pl.BlockSpec((tm, a), lambda i: (i, 0)),
                  pl.BlockSpec((None, a, d), lambda i: (layer, 0, 0))],
        out_specs=pl.BlockSpec((tm, d), lambda i: (i, 0)),
        out_shape=jax.ShapeDtypeStruct((m, d), F32),
        compiler_params=_params(("arbitrary",)),
        name="out_proj",
    )(x2, o2, w_all)


def _ffn_kernel(x_ref, g_ref, wg_ref, wu_ref, wd_ref, gf_ref, y_ref, h_ref, *, final_norm):
    f = pl.program_id(1)

    @pl.when(f == 0)
    def _():
        x = x_ref[...]
        h_ref[...] = _rms_rows(x, g_ref[...], RMS_EPS).astype(BF16)
        y_ref[...] = x

    h = h_ref[...]
    gate = _dot(h, wg_ref[...])
    up = _dot(h, wu_ref[...])
    act = (gate * jax.nn.sigmoid(gate) * up).astype(BF16)
    y_ref[...] += _dot(act, wd_ref[...])

    if final_norm:
        @pl.when(f == pl.num_programs(1) - 1)
        def _():
            y_ref[...] = _rms_rows(y_ref[...], gf_ref[...], RMS_EPS)


def _ffn(x2, g, w_gate_up, w_down, layer, g_final, *, final_norm, tm=512, tf=512):
    m, d = x2.shape
    d_ff = w_down.shape[1]
    nf = d_ff // tf
    return pl.pallas_call(
        functools.partial(_ffn_kernel, final_norm=final_norm),
        grid=(m // tm, nf),
        in_specs=[pl.BlockSpec((tm, d), lambda i, f: (i, 0)),
                  pl.BlockSpec((1, d), lambda i, f: (0, 0)),
                  pl.BlockSpec((None, d, tf), lambda i, f: (layer, 0, f)),
                  pl.BlockSpec((None, d, tf), lambda i, f: (layer, 0, f + nf)),
                  pl.BlockSpec((None, tf, d), lambda i, f: (layer, f, 0)),
                  pl.BlockSpec((1, d), lambda i, f: (0, 0))],
        out_specs=pl.BlockSpec((tm, d), lambda i, f: (i, 0)),
        out_shape=jax.ShapeDtypeStruct((m, d), F32),
        scratch_shapes=[pltpu.VMEM((tm, d), BF16)],
        compiler_params=_params(("arbitrary", "arbitrary")),
        name="ffn_final" if final_norm else "ffn",
    )(x2, g.reshape(1, d), w_gate_up, w_gate_up, w_down, g_final.reshape(1, d))


def _qk_colscale(scale):
    return jnp.concatenate([jnp.full((ATTN_WIDTH,), scale * LOG2E, F32), jnp.ones((2 * ATTN_WIDTH,), F32)])


def kernel(x, norm_mix, fox_w_in, fox_b_f, fox_w_out, sb_w_in, sb_w_out, diff_w_in, diff_lambda_q1, diff_lambda_k1, diff_lambda_q2, diff_lambda_k2, diff_subln, diff_w_out, norm_ffn, ffn_w_gate_up, ffn_w_down, final_norm):
    b, seq, d = x.shape
    depth = norm_mix.shape[0]
    a3 = 3 * ATTN_WIDTH
    fox_w = fox_w_in[:, :, :a3].astype(BF16)
    fox_w_gate = jnp.pad(fox_w_in[:, :, a3:], ((0, 0), (0, 0), (0, LANES - N_HEADS))).astype(BF16)
    sb_w, diff_w = sb_w_in.astype(BF16), diff_w_in.astype(BF16)
    w_out = {0: fox_w_out.astype(BF16), 1: sb_w_out.astype(BF16), 2: diff_w_out.astype(BF16)}
    w_gate_up, w_down = ffn_w_gate_up.astype(BF16), ffn_w_down.astype(BF16)
    x2 = x.reshape(b * seq, d)
    for i in range(depth):
        kind, j = i % N_MIXERS, i // N_MIXERS
        if kind == 0:
            proj, gate_logits = _norm_proj(x2, norm_mix[i], fox_w, j, _qk_colscale(HEAD_DIM ** -0.5),
                                           fox_w_gate[j])
            b_f_row = jnp.pad(fox_b_f[j].astype(F32), (0, LANES - N_HEADS)).reshape(1, LANES)
            cum_t = _gate_cumsum(gate_logits, b_f_row, seq)
            o = _fox_attention(proj.reshape(b, seq, a3), cum_t)
        elif kind == 1:
            proj = _norm_proj(x2, norm_mix[i], sb_w, j, _qk_colscale(HEAD_DIM ** -0.5))[0]
            o = _sb_attention(proj.reshape(b, seq, a3))
        else:
            lambda_init = 0.8 - 0.6 * float(np.exp(-0.3 * i))
            proj = _norm_proj(x2, norm_mix[i], diff_w, j, _qk_colscale(DIFF_QK_DIM ** -0.5))[0]
            o = _diff_attention(proj.reshape(b, seq, a3), diff_lambda_q1[j], diff_lambda_k1[j],
                                diff_lambda_q2[j], diff_lambda_k2[j], diff_subln[j], lambda_init)
        x2 = _out_proj(x2, o.reshape(b * seq, ATTN_WIDTH), w_out[kind], j)
        x2 = _ffn(x2, norm_ffn[i], w_gate_up, w_down, i, final_norm, final_norm=(i == depth - 1))
    return x2.reshape(b, seq, d)
```
